```python
import jax, jax.numpy as jnp
from jax import lax
import numpy as np

D_MODEL = 1024
BATCH = 2
SEQ = 16384
DEPTH = 2
DEC_BATCH = 8
DEC_SEQ = 16
PAST_LEN = 1024

CHUNK = 64
N_A_LAYERS = DEPTH // 2
N_B_LAYERS = DEPTH - N_A_LAYERS
A_CHUNK = 128
A_WIDTH = 2 * D_MODEL
A_GROUPS = 8
N_HEADS = 16
HEAD_DIM = D_MODEL // N_HEADS
BAND_CHUNKS = 8
BAND_PAST = BAND_CHUNKS * CHUNK
REL_CLIP = 256
N_GROUPS = 4
EXPERTS_PER_GROUP = 4
N_EXPERTS = N_GROUPS * EXPERTS_PER_GROUP
TOP_K_IN_GROUP = 2
D_EXPERT = D_MODEL // 4
ALPHA = (2 * DEPTH) ** 0.25
BETA = (8 * DEPTH) ** -0.25
LN_EPS = 1e-5
NEG_INF = -1e30

kernel_name = 'yoco_chunkmlp_bandattn_hmoe_stream_step'


def _layernorm(x, g, b):
    xf = x.astype(jnp.float32)
    mu = jnp.mean(xf, axis=-1, keepdims=True)
    var = jnp.mean(jnp.square(xf - mu), axis=-1, keepdims=True)
    y = (xf - mu) * lax.rsqrt(var + LN_EPS)
    return (y * g.astype(jnp.float32) + b.astype(jnp.float32)).astype(x.dtype)


def _ada(c, w, b):
    m = (jax.nn.silu(c) @ w + b).reshape(c.shape[0], 6, 1, D_MODEL)
    return tuple(m[:, i] for i in range(6))


def _chunk_mlp(xm, w_in, vn_g, vn_b, w_s, b_s, w_out):
    bsz, s, _ = xm.shape
    lc = min(A_CHUNK, s)
    z = jax.nn.gelu(xm @ w_in)
    u, v = jnp.split(z, 2, axis=-1)
    v = _layernorm(v, vn_g, vn_b)
    mask = jnp.tril(jnp.ones((lc, lc), dtype=bool))
    ws = jnp.where(mask, w_s[:, :lc, :lc], 0).astype(v.dtype)
    vg = v.reshape(bsz, s // lc, lc, A_GROUPS, A_WIDTH // A_GROUPS)
    mixed = jnp.einsum('gts,bnsge->bntge', ws, vg) + jnp.transpose(b_s[:, :lc])[:, :, None].astype(v.dtype)
    gated = u * mixed.reshape(bsz, s, A_WIDTH)
    return gated @ w_out, v


def _rel_bias(table, qpos, kpos):
    rel = jnp.clip(qpos[:, None] - kpos[None, :], -REL_CLIP, REL_CLIP) + REL_CLIP
    return table[:, rel].astype(jnp.float32)


def _band_attend(q, k, v, bias, kvalid):
    s = jnp.einsum('bqhd,bkhd->bhqk', q, k).astype(jnp.float32) * (HEAD_DIM ** -0.5) + bias
    s = jnp.where(kvalid, s, NEG_INF)
    p = jax.nn.softmax(s, axis=-1).astype(v.dtype)
    return jnp.einsum('bhqk,bkhd->bqhd', p, v)


def _hier_moe(x, w_rg, b_rg, w_re, b_re, w1, w3, w2):
    shp = x.shape
    t = x.reshape(-1, D_MODEL)
    lg = (t @ w_rg + b_rg).astype(jnp.float32)
    pg = jax.nn.softmax(lg, axis=-1)
    gi = jnp.argmax(lg, axis=-1)
    pg_sel = jnp.take_along_axis(pg, gi[:, None], axis=-1)
    le = (t @ w_re + b_re).astype(jnp.float32).reshape(-1, N_GROUPS, EXPERTS_PER_GROUP)
    le_sel = jnp.take_along_axis(le, gi[:, None, None], axis=1)[:, 0]
    top_v, top_i = lax.top_k(le_sel, TOP_K_IN_GROUP)
    w_top = jax.nn.softmax(top_v, axis=-1)
    within = jnp.sum(jax.nn.one_hot(top_i, EXPERTS_PER_GROUP, dtype=jnp.float32) * w_top[..., None], axis=1)
    gate = (jax.nn.one_hot(gi, N_GROUPS, dtype=jnp.float32)[:, :, None]
            * (pg_sel[:, :, None] * within[:, None, :])).reshape(-1, N_EXPERTS).astype(x.dtype)
    out = jnp.zeros_like(t)
    for e in range(N_EXPERTS):
        h = jax.nn.silu(t @ w1[e]) * (t @ w3[e])
        out = out + gate[:, e:e + 1] * (h @ w2[e])
    return out.reshape(shp)


def setup_inputs(seed: int = 0) -> dict:
    key = jax.random.key(seed)
    ks = iter(jax.random.split(key, 40))
    f32 = jnp.float32
    nrm = lambda shape, s: (jax.random.normal(next(ks), shape, f32) * s)
    cache_len = min(BAND_PAST, PAST_LEN)
    d = D_MODEL
    return {
        'x_prompt': nrm((BATCH, SEQ, d), 1.0),
        'x_sample': nrm((DEC_BATCH, DEC_SEQ, d), 1.0),
        'cache_k': nrm((DEC_BATCH, cache_len, N_HEADS, HEAD_DIM), 1.0),
        'cache_v': nrm((DEC_BATCH, cache_len, N_HEADS, HEAD_DIM), BETA),
        'c_prompt': nrm((BATCH, d), 1.0),
        'c_sample': nrm((DEC_BATCH, d), 1.0),
        'ada_w': nrm((DEPTH, d, 6 * d), 0.1 * d ** -0.5),
        'ada_b': nrm((DEPTH, 6 * d), 0.01),
        'ln_g': 1.0 + nrm((DEPTH, 2, d), 0.01),
        'ln_b': nrm((DEPTH, 2, d), 0.01),
        'a_w_in': nrm((N_A_LAYERS, d, 2 * A_WIDTH), d ** -0.5),
        'a_vn_g': 1.0 + nrm((N_A_LAYERS, A_WIDTH), 0.01),
        'a_vn_b': nrm((N_A_LAYERS, A_WIDTH), 0.01),
        'a_w_s': nrm((N_A_LAYERS, A_GROUPS, A_CHUNK, A_CHUNK), A_CHUNK ** -0.5),
        'a_b_s': 1.0 + nrm((N_A_LAYERS, A_GROUPS, A_CHUNK), 0.1),
        'a_w_out': nrm((N_A_LAYERS, A_WIDTH, d), BETA * A_WIDTH ** -0.5),
        'kv_w_k': nrm((d, N_HEADS * HEAD_DIM), d ** -0.5),
        'kv_w_v': nrm((d, N_HEADS * HEAD_DIM), BETA * d ** -0.5),
        'b_w_q': nrm((N_B_LAYERS, d, N_HEADS * HEAD_DIM), d ** -0.5),
        'b_rel': nrm((N_B_LAYERS, N_HEADS, 2 * REL_CLIP + 1), 0.1),
        'b_w_o': nrm((N_B_LAYERS, N_HEADS * HEAD_DIM, d), BETA * d ** -0.5),
        'moe_w_rg': nrm((DEPTH, d, N_GROUPS), d ** -0.5),
        'moe_b_rg': nrm((DEPTH, N_GROUPS), 0.01),
        'moe_w_re': nrm((DEPTH, d, N_EXPERTS), d ** -0.5),
        'moe_b_re': nrm((DEPTH, N_EXPERTS), 0.01),
        'moe_w1': nrm((DEPTH, N_EXPERTS, d, D_EXPERT), d ** -0.5),
        'moe_w3': nrm((DEPTH, N_EXPERTS, d, D_EXPERT), d ** -0.5),
        'moe_w2': nrm((DEPTH, N_EXPERTS, D_EXPERT, d), BETA * D_EXPERT ** -0.5),
    }


def reference(x_prompt, x_sample, cache_k, cache_v, c_prompt, c_sample,
              ada_w, ada_b, ln_g, ln_b,
              a_w_in, a_vn_g, a_vn_b, a_w_s, a_b_s, a_w_out,
              kv_w_k, kv_w_v, b_w_q, b_rel, b_w_o,
              moe_w_rg, moe_b_rg, moe_w_re, moe_b_re, moe_w1, moe_w3, moe_w2):

    def run(x, c, attend):
        bsz, s, _ = x.shape
        v_rows = []
        k_sh = None
        v_sh = None
        for layer in range(DEPTH):
            sh1, sc1, g1, sh2, sc2, g2 = _ada(c, ada_w[layer], ada_b[layer])
            xm = x * (1 + sc1) + sh1
            if layer < N_A_LAYERS:
                out, va = _chunk_mlp(xm, a_w_in[layer], a_vn_g[layer], a_vn_b[layer],
                                     a_w_s[layer], a_b_s[layer], a_w_out[layer])
                v_rows.append(va)
            else:
                j = layer - N_A_LAYERS
                if k_sh is None:
                    k_sh = (x @ kv_w_k).reshape(bsz, s, N_HEADS, HEAD_DIM)
                    v_sh = (x @ kv_w_v).reshape(bsz, s, N_HEADS, HEAD_DIM)
                q = (xm @ b_w_q[j]).reshape(bsz, s, N_HEADS, HEAD_DIM)
                o = attend(q, k_sh, v_sh, b_rel[j])
                out = o.reshape(bsz, s, N_HEADS * HEAD_DIM) @ b_w_o[j]
            x = _layernorm(ALPHA * x + (1 + g1) * out, ln_g[layer, 0], ln_b[layer, 0])
            xm = x * (1 + sc2) + sh2
            out = _hier_moe(xm, moe_w_rg[layer], moe_b_rg[layer], moe_w_re[layer], moe_b_re[layer],
                            moe_w1[layer], moe_w3[layer], moe_w2[layer])
            x = _layernorm(ALPHA * x + (1 + g2) * out, ln_g[layer, 1], ln_b[layer, 1])
        return x, k_sh, v_sh, v_rows

    def attend_prompt(q, k, v, rel_table):
        bsz, s = q.shape[0], q.shape[1]
        nc = s // CHUNK
        band = BAND_PAST + CHUNK
        pad = ((0, 0), (BAND_PAST, 0), (0, 0), (0, 0))
        kp = jnp.pad(k, pad)
        vp = jnp.pad(v, pad)
        qi = jnp.arange(CHUNK)
        kj = jnp.arange(band)
        bias = _rel_bias(rel_table, qi + BAND_PAST, kj)
        qc = jnp.moveaxis(q.reshape(bsz, nc, CHUNK, N_HEADS, HEAD_DIM), 1, 0)

        def one(args):
            ci, qb = args
            start = ci * CHUNK
            kb = lax.dynamic_slice_in_dim(kp, start, band, axis=1)
            vb = lax.dynamic_slice_in_dim(vp, start, band, axis=1)
            kvalid = (start - BAND_PAST + kj) >= 0
            return _band_attend(qb, kb, vb, bias, kvalid)

        o = lax.map(one, (jnp.arange(nc), qc))
        return jnp.moveaxis(o, 0, 1).reshape(bsz, s, N_HEADS, HEAD_DIM)

    def attend_sample(q, k, v, rel_table):
        lp = cache_k.shape[1]
        l = q.shape[1]
        kf = jnp.concatenate([cache_k.astype(k.dtype), k], axis=1)
        vf = jnp.concatenate([cache_v.astype(v.dtype), v], axis=1)
        bias = _rel_bias(rel_table, lp + jnp.arange(l), jnp.arange(lp + l))
        kvalid = jnp.ones((lp + l,), dtype=bool)
        return _band_attend(q, kf, vf, bias, kvalid)

    y_prompt, k_p, v_p, _ = run(x_prompt, c_prompt, attend_prompt)
    y_sample, new_k_sample, new_v_sample, va_s = run(x_sample, c_sample, attend_sample)
    keep = min(BAND_PAST, x_prompt.shape[1])
    new_k_prompt = k_p[:, -keep:]
    new_v_prompt = v_p[:, -keep:]
    new_va_sample = jnp.stack(va_s, axis=0)
    return (y_prompt, y_sample, new_k_prompt, new_v_prompt, new_k_sample, new_v_sample, new_va_sample)
```

```python
import functools

import jax
import jax.numpy as jnp
from jax import lax
from jax.experimental import pallas as pl
from jax.experimental.pallas import tpu as pltpu

D_MODEL = 1024
DEPTH = 2
A_CHUNK = 128
A_WIDTH = 2 * D_MODEL
A_GROUPS = 8
A_GROUP_WIDTH = A_WIDTH // A_GROUPS
N_HEADS = 16
HEAD_DIM = D_MODEL // N_HEADS
CHUNK = 64
BAND_PAST = 8 * CHUNK
REL_CLIP = 256
N_GROUPS = 4
EXPERTS_PER_GROUP = 4
N_EXPERTS = N_GROUPS * EXPERTS_PER_GROUP
N_PAIRS = 6
N_BUCKETS = N_GROUPS * N_PAIRS
D_EXPERT = D_MODEL // 4
ALPHA = (2 * DEPTH) ** 0.25
LN_EPS = 1e-5
NEG_INF = -1e30

LANES = 128
ROUTE_ROWS = 8
EXPERT_LOGIT_ROW = 8
ATT_Q = 128
ATT_K = BAND_PAST + ATT_Q
VMEM_LIMIT = 56 * 1024 * 1024

F32 = jnp.float32
BF16 = jnp.bfloat16


def _params(n_axes):
    return pltpu.CompilerParams(dimension_semantics=("arbitrary",) * n_axes,
                                vmem_limit_bytes=VMEM_LIMIT)


def _const_spec(shape):
    zeros = (0,) * len(shape)
    return pl.BlockSpec(shape, lambda *_: zeros)


def _layernorm(x, g, b):
    mu = jnp.mean(x, axis=-1, keepdims=True)
    xc = x - mu
    var = jnp.mean(xc * xc, axis=-1, keepdims=True)
    return xc * lax.rsqrt(var + LN_EPS) * g + b


def _dot(a, b):
    return jnp.dot(a, b, preferred_element_type=F32)


def _ada_kernel(c_ref, w_ref, b_ref, o_ref):
    c = c_ref[...]
    a = c * (1.0 / (1.0 + jnp.exp(-c)))
    o_ref[...] = jnp.dot(a, w_ref[...], precision=lax.Precision.HIGHEST,
                         preferred_element_type=F32) + b_ref[...]


def _ada(c_all, ada_w, ada_b):
    nb = c_all.shape[0]
    n_out = ada_w.shape[-1]
    bn = D_MODEL
    return pl.pallas_call(
        _ada_kernel,
        grid=(DEPTH, n_out // bn),
        in_specs=[pl.BlockSpec((nb, D_MODEL), lambda l, j: (0, 0)),
                  pl.BlockSpec((None, D_MODEL, bn), lambda l, j: (l, 0, j)),
                  pl.BlockSpec((None, 1, bn), lambda l, j: (l, 0, j))],
        out_specs=pl.BlockSpec((None, nb, bn), lambda l, j: (l, 0, j)),
        out_shape=jax.ShapeDtypeStruct((DEPTH, nb, n_out), F32),
        compiler_params=_params(2),
    )(c_all, ada_w, ada_b.reshape(DEPTH, 1, n_out))


def _route(xm, wr_ref, br_ref):
    logits = jnp.dot(xm, wr_ref[...], precision=lax.Precision.HIGHEST,
                     preferred_element_type=F32) + br_ref[...]
    t = logits.T
    lg = [t[g:g + 1] for g in range(N_GROUPS)]
    m = jnp.maximum(jnp.maximum(lg[0], lg[1]), jnp.maximum(lg[2], lg[3]))
    den = sum(jnp.exp(l - m) for l in lg)
    pg_sel = 1.0 / den
    gi = jnp.where(lg[0] == m, 0, jnp.where(lg[1] == m, 1, jnp.where(lg[2] == m, 2, 3)))
    le = []
    for k in range(EXPERTS_PER_GROUP):
        rows = [t[EXPERT_LOGIT_ROW + EXPERTS_PER_GROUP * g + k:
                  EXPERT_LOGIT_ROW + EXPERTS_PER_GROUP * g + k + 1] for g in range(N_GROUPS)]
        le.append(jnp.where(gi == 0, rows[0], jnp.where(gi == 1, rows[1],
                                                      jnp.where(gi == 2, rows[2], rows[3]))))
    v1 = jnp.maximum(jnp.maximum(le[0], le[1]), jnp.maximum(le[2], le[3]))
    i1 = jnp.where(le[0] == v1, 0, jnp.where(le[1] == v1, 1, jnp.where(le[2] == v1, 2, 3)))
    rest = [jnp.where(i1 == k, -jnp.inf, le[k]) for k in range(EXPERTS_PER_GROUP)]
    v2 = jnp.maximum(jnp.maximum(rest[0], rest[1]), jnp.maximum(rest[2], rest[3]))
    cand = [(rest[k] == v2) & (i1 != k) for k in range(EXPERTS_PER_GROUP)]
    i2 = jnp.where(cand[0], 0, jnp.where(cand[1], 1, jnp.where(cand[2], 2, 3)))
    e = jnp.exp(v2 - v1)
    w1 = 1.0 / (1.0 + e)
    w2 = e / (1.0 + e)
    first_is_lo = i1 < i2
    lo = jnp.where(first_is_lo, i1, i2)
    hi = jnp.where(first_is_lo, i2, i1)
    g_lo = pg_sel * jnp.where(first_is_lo, w1, w2)
    g_hi = pg_sel * jnp.where(first_is_lo, w2, w1)
    pair = jnp.where(lo == 0, 0, jnp.where(lo == 1, 3, 5)) + hi - lo - 1
    bucket = (gi * N_PAIRS + pair).astype(F32)
    rec = lax.broadcasted_iota(jnp.int32, (ROUTE_ROWS, bucket.shape[1]), 0)
    return jnp.where(rec == 0, bucket, jnp.where(rec == 1, g_lo, jnp.where(rec == 2, g_hi, 0.0)))


def _mod_rows(mod_ref):
    return [mod_ref[k:k + 1, :] for k in range(6)]


def _mixer_a_kernel(x_ref, mod_ref, w_in_ref, vng_ref, vnb_ref, ws_ref, bs_ref, w_out_ref,
                    lng_ref, lnb_ref, wr_ref, br_ref, x1_ref, xm2_ref, route_ref, *va_ref):
    tm = x_ref.shape[0]
    x = x_ref[...]
    sh1, sc1, g1, sh2, sc2, _ = _mod_rows(mod_ref)
    xm = (x * (1.0 + sc1) + sh1).astype(BF16)
    zv = _dot(xm, w_in_ref[:, A_WIDTH:])
    v = _layernorm(jax.nn.gelu(zv), vng_ref[...], vnb_ref[...])
    if va_ref:
        va_ref[0][...] = v
    vb = v.astype(BF16)
    row = lax.broadcasted_iota(jnp.int32, (A_CHUNK, A_CHUNK), 0)
    col = lax.broadcasted_iota(jnp.int32, (A_CHUNK, A_CHUNK), 1)
    acc = jnp.zeros((tm, D_MODEL), F32)
    for g in range(A_GROUPS):
        cols = slice(g * A_GROUP_WIDTH, (g + 1) * A_GROUP_WIDTH)
        u = jax.nn.gelu(_dot(xm, w_in_ref[:, cols]))
        ws = jnp.where(col <= row, ws_ref[g], 0.0).astype(BF16)
        bias = bs_ref[:, g:g + 1]
        mixed = [_dot(ws, vb[c * A_CHUNK:(c + 1) * A_CHUNK, cols]) + bias
                 for c in range(tm // A_CHUNK)]
        mixed = jnp.concatenate(mixed, axis=0) if len(mixed) > 1 else mixed[0]
        acc = acc + _dot((u * mixed).astype(BF16), w_out_ref[cols, :])
    x1 = _layernorm(ALPHA * x + (1.0 + g1) * acc, lng_ref[...], lnb_ref[...])
    xm2 = x1 * (1.0 + sc2) + sh2
    x1_ref[...] = x1
    xm2_ref[...] = xm2
    route_ref[...] = _route(xm2, wr_ref, br_ref)


def _mixer_a(x, mod, w_in, vn_g, vn_b, w_s, b_s_t, w_out, ln_g, ln_b, w_r, b_r, *,
             rows_per_batch, tm, emit_va):
    t = x.shape[0]
    tpb = rows_per_batch // tm
    row_spec = pl.BlockSpec((tm, D_MODEL), lambda i: (i, 0))
    out_specs = [row_spec, row_spec, pl.BlockSpec((ROUTE_ROWS, tm), lambda i: (0, i))]
    out_shape = [jax.ShapeDtypeStruct((t, D_MODEL), F32), jax.ShapeDtypeStruct((t, D_MODEL), F32),
                 jax.ShapeDtypeStruct((ROUTE_ROWS, t), F32)]
    if emit_va:
        out_specs.append(pl.BlockSpec((tm, A_WIDTH), lambda i: (i, 0)))
        out_shape.append(jax.ShapeDtypeStruct((t, A_WIDTH), F32))
    return pl.pallas_call(
        _mixer_a_kernel,
        grid=(t // tm,),
        in_specs=[row_spec,
                  pl.BlockSpec((None, 6, D_MODEL), lambda i: (i // tpb, 0, 0)),
                  _const_spec((D_MODEL, 2 * A_WIDTH)),
                  _const_spec((1, A_WIDTH)), _const_spec((1, A_WIDTH)),
                  _const_spec((A_GROUPS, A_CHUNK, A_CHUNK)),
                  _const_spec((A_CHUNK, A_GROUPS)),
                  _const_spec((A_WIDTH, D_MODEL)),
                  _const_spec((1, D_MODEL)), _const_spec((1, D_MODEL)),
                  _const_spec((D_MODEL, LANES)), _const_spec((1, LANES))],
        out_specs=out_specs,
        out_shape=out_shape,
        compiler_params=_params(1),
    )(x, mod, w_in, vn_g, vn_b, w_s, b_s_t, w_out, ln_g, ln_b, w_r, b_r)


def _gather_kernel(idx_ref, src_ref, out_ref, sem):
    rows = out_ref.shape[0]
    base = pl.program_id(0) * rows

    def issue(r, carry):
        pltpu.make_async_copy(src_ref.at[pl.ds(idx_ref[base + r], 1), :],
                              out_ref.at[pl.ds(r, 1), :], sem).start()
        return carry

    lax.fori_loop(0, rows, issue, 0)

    def drain(r, carry):
        pltpu.make_async_copy(src_ref.at[pl.ds(0, 1), :], out_ref.at[pl.ds(r, 1), :], sem).wait()
        return carry

    lax.fori_loop(0, rows, drain, 0)


def _gather_rows(src, idx, *, rows):
    m = idx.shape[0]
    width = src.shape[1]
    return pl.pallas_call(
        _gather_kernel,
        grid_spec=pltpu.PrefetchScalarGridSpec(
            num_scalar_prefetch=1,
            grid=(m // rows,),
            in_specs=[pl.BlockSpec(memory_space=pl.ANY)],
            out_specs=pl.BlockSpec((rows, width), lambda i, idx: (i, 0)),
            scratch_shapes=[pltpu.SemaphoreType.DMA]),
        out_shape=jax.ShapeDtypeStruct((m, width), src.dtype),
        compiler_params=_params(1),
    )(idx, src)


def _moe_kernel(e_lo_ref, e_hi_ref, valid_ref, xs_ref, gate_ref,
                w1a_ref, w3a_ref, w2a_ref, w1b_ref, w3b_ref, w2b_ref, o_ref):
    i = pl.program_id(0)

    @pl.when(valid_ref[i] != 0)
    def _():
        x = xs_ref[...].astype(BF16)

        def expert(w1_ref, w3_ref, w2_ref, gate):
            a = _dot(x, w1_ref[...])
            b = _dot(x, w3_ref[...])
            h = (a * (1.0 / (1.0 + jnp.exp(-a))) * b).astype(BF16)
            return gate * _dot(h, w2_ref[...])

        out = expert(w1a_ref, w3a_ref, w2a_ref, gate_ref[:, 0:1])
        o_ref[...] = out + expert(w1b_ref, w3b_ref, w2b_ref, gate_ref[:, 1:2])

    @pl.when(valid_ref[i] == 0)
    def _():
        o_ref[...] = jnp.zeros_like(o_ref)


def _moe_sorted(xs, gates, e_lo, e_hi, valid, w1, w3, w2, *, tm):
    p = xs.shape[0]
    w_up = lambda tbl: pl.BlockSpec((None, D_MODEL, D_EXPERT),
                                    lambda i, lo, hi, v: ((lo, hi)[tbl][i], 0, 0))
    w_dn = lambda tbl: pl.BlockSpec((None, D_EXPERT, D_MODEL),
                                    lambda i, lo, hi, v: ((lo, hi)[tbl][i], 0, 0))
    return pl.pallas_call(
        _moe_kernel,
        grid_spec=pltpu.PrefetchScalarGridSpec(
            num_scalar_prefetch=3,
            grid=(p // tm,),
            in_specs=[pl.BlockSpec((tm, D_MODEL), lambda i, lo, hi, v: (i, 0)),
                      pl.BlockSpec((tm, 2), lambda i, lo, hi, v: (i, 0)),
                      w_up(0), w_up(0), w_dn(0), w_up(1), w_up(1), w_dn(1)],
            out_specs=pl.BlockSpec((tm, D_MODEL), lambda i, lo, hi, v: (i, 0))),
        out_shape=jax.ShapeDtypeStruct((p, D_MODEL), F32),
        compiler_params=_params(1),
    )(e_lo, e_hi, valid, xs, gates, w1, w3, w2, w1, w3, w2)


_PAIR_LO = (0, 0, 0, 1, 1, 2)
_PAIR_HI = (1, 2, 3, 2, 3, 3)


def _hier_moe(xm2, route, w1, w3, w2, *, tm):
    t = xm2.shape[0]
    n_tiles = t // tm + N_BUCKETS
    bucket = route[0].astype(jnp.int32)
    onehot = (bucket[:, None] == jnp.arange(N_BUCKETS)[None, :]).astype(jnp.int32)
    csum = jnp.cumsum(onehot, axis=0)
    rank = jnp.sum(onehot * csum, axis=1) - 1
    counts = csum[-1]
    padded = ((counts + tm - 1) // tm) * tm
    ends = jnp.cumsum(padded)
    pos = (ends - padded)[bucket] + rank
    src = jnp.zeros((n_tiles * tm,), jnp.int32).at[pos].set(jnp.arange(t, dtype=jnp.int32))
    tile_start = jnp.arange(n_tiles, dtype=jnp.int32) * tm
    tile_bucket = jnp.minimum(jnp.searchsorted(ends, tile_start, side="right"), N_BUCKETS - 1)
    valid = (tile_start < ends[-1]).astype(jnp.int32)
    group = tile_bucket // N_PAIRS
    pair = tile_bucket % N_PAIRS
    e_lo = (group * EXPERTS_PER_GROUP + jnp.asarray(_PAIR_LO, jnp.int32)[pair]).astype(jnp.int32)
    e_hi = (group * EXPERTS_PER_GROUP + jnp.asarray(_PAIR_HI, jnp.int32)[pair]).astype(jnp.int32)
    gates = jnp.transpose(route[1:3])[src]
    xs = _gather_rows(xm2, src, rows=tm)
    ys = _moe_sorted(xs, gates, e_lo, e_hi, valid, w1, w3, w2, tm=tm)
    return _gather_rows(ys, pos.astype(jnp.int32), rows=tm)


def _ln_qkv_kernel(x1_ref, y_ref, mod0_ref, mod1_ref, lng_ref, lnb_ref, wq_ref, wk_ref, wv_ref,
                   x2_ref, q_ref, k_ref, v_ref, k32_ref, v32_ref):
    g2 = mod0_ref[5:6, :]
    sh1, sc1 = mod1_ref[0:1, :], mod1_ref[1:2, :]
    x2 = _layernorm(ALPHA * x1_ref[...] + (1.0 + g2) * y_ref[...], lng_ref[...], lnb_ref[...])
    x2_ref[...] = x2
    xb = x2.astype(BF16)
    k = _dot(xb, wk_ref[...])
    v = _dot(xb, wv_ref[...])
    k32_ref[...] = k
    v32_ref[...] = v
    k_ref[...] = k.astype(BF16)
    v_ref[...] = v.astype(BF16)
    xm = (x2 * (1.0 + sc1) + sh1).astype(BF16)
    q_ref[...] = _dot(xm, wq_ref[...]).astype(BF16)


def _ln_qkv(x1, y, mod0, mod1, ln_g, ln_b, w_q, w_k, w_v, *, rows_per_batch, tm, keep):
    t = x1.shape[0]
    nb = t // rows_per_batch
    tpb = rows_per_batch // tm
    first_kept = tpb - keep // tm
    row_spec = pl.BlockSpec((tm, D_MODEL), lambda i: (i, 0))
    mod_spec = pl.BlockSpec((None, 6, D_MODEL), lambda i: (i // tpb, 0, 0))
    w_spec = _const_spec((D_MODEL, D_MODEL))
    keep_spec = pl.BlockSpec((None, tm, D_MODEL),
                             lambda i: (i // tpb, jnp.maximum(i % tpb - first_kept, 0), 0))
    return pl.pallas_call(
        _ln_qkv_kernel,
        grid=(t // tm,),
        in_specs=[row_spec, row_spec, mod_spec, mod_spec,
                  _const_spec((1, D_MODEL)), _const_spec((1, D_MODEL)), w_spec, w_spec, w_spec],
        out_specs=[row_spec, row_spec, row_spec, row_spec, keep_spec, keep_spec],
        out_shape=[jax.ShapeDtypeStruct((t, D_MODEL), F32)] +
                  [jax.ShapeDtypeStruct((t, D_MODEL), BF16)] * 3 +
                  [jax.ShapeDtypeStruct((nb, keep, D_MODEL), F32)] * 2,
        compiler_params=_params(1),
    )(x1, y, mod0, mod1, ln_g, ln_b, w_q, w_k, w_v)


def _attn_kernel(q_ref, kp_ref, kc_ref, vp_ref, vc_ref, bias_ref, o_ref, kbuf, vbuf, *,
                 tile_axis, mask_first):
    tq = q_ref.shape[0]
    kbuf[0:BAND_PAST, :] = kp_ref[...]
    kbuf[BAND_PAST:, :] = kc_ref[...]
    vbuf[0:BAND_PAST, :] = vp_ref[...]
    vbuf[BAND_PAST:, :] = vc_ref[...]
    lane = lax.broadcasted_iota(jnp.int32, (ATT_Q, LANES), 1)
    head0 = lane < HEAD_DIM
    key_pos = lax.broadcasted_iota(jnp.int32, (ATT_Q, ATT_K), 1)
    n_hidden = jnp.where(pl.program_id(tile_axis) == 0, BAND_PAST, 0)
    for j in range(tq // ATT_Q):
        q = q_ref[j * ATT_Q:(j + 1) * ATT_Q, :]
        kw = kbuf[j * ATT_Q:j * ATT_Q + ATT_K, :]
        vw = vbuf[j * ATT_Q:j * ATT_Q + ATT_K, :]
        outs = []
        for h in range(2):
            qh = jnp.where(head0 if h == 0 else ~head0, q, jnp.zeros_like(q))
            s = lax.dot_general(qh, kw, (((1,), (1,)), ((), ())), preferred_element_type=F32)
            s = s * (HEAD_DIM ** -0.5) + bias_ref[h]
            if mask_first:
                s = jnp.where(key_pos + j * ATT_Q < n_hidden, NEG_INF, s)
            s_max = jnp.max(s, axis=-1, keepdims=True)
            p = jnp.exp(s - s_max)
            p = p / jnp.sum(p, axis=-1, keepdims=True)
            outs.append(_dot(p.astype(BF16), vw))
        o_ref[j * ATT_Q:(j + 1) * ATT_Q, :] = jnp.where(head0, outs[0], outs[1]).astype(o_ref.dtype)


def _attention(q, k_prev, k_cur, v_prev, v_cur, bias, *, tq, mask_first):
    nb, s, _ = q.shape
    n_hp = D_MODEL // LANES
    cur_spec = pl.BlockSpec((None, tq, LANES), lambda b, i, h: (b, i, h))
    if mask_first:
        prev_spec = pl.BlockSpec((None, BAND_PAST, LANES),
                                 lambda b, i, h: (b, jnp.maximum(i * (tq // BAND_PAST) - 1, 0), h))
    else:
        prev_spec = pl.BlockSpec((None, BAND_PAST, LANES), lambda b, i, h: (b, 0, h))
    return pl.pallas_call(
        functools.partial(_attn_kernel, tile_axis=1, mask_first=mask_first),
        grid=(nb, s // tq, n_hp),
        in_specs=[cur_spec, prev_spec, cur_spec, prev_spec, cur_spec,
                  pl.BlockSpec((2, ATT_Q, ATT_K), lambda b, i, h: (h, 0, 0))],
        out_specs=cur_spec,
        out_shape=jax.ShapeDtypeStruct((nb, s, D_MODEL), BF16),
        scratch_shapes=[pltpu.VMEM((BAND_PAST + tq, LANES), BF16),
                        pltpu.VMEM((BAND_PAST + tq, LANES), BF16)],
        compiler_params=_params(3),
    )(q, k_prev, k_cur, v_prev, v_cur, bias)


def _attn_bias(table, *, n_valid_keys):
    qi = jnp.arange(ATT_Q)[:, None]
    kj = jnp.arange(ATT_K)[None, :]
    rel = jnp.clip(BAND_PAST + qi - kj, -REL_CLIP, REL_CLIP) + REL_CLIP
    bias = table[:, rel].astype(F32)
    if n_valid_keys is None:
        qc, kc = qi // CHUNK, kj // CHUNK
        visible = (kc >= qc) & (kc <= qc + BAND_PAST // CHUNK)
    else:
        visible = jnp.broadcast_to(kj < n_valid_keys, (ATT_Q, ATT_K))
    return jnp.where(visible[None], bias, NEG_INF)


def _oproj_kernel(o_ref, x2_ref, mod_ref, wo_ref, lng_ref, lnb_ref, wr_ref, br_ref,
                  x3_ref, xm2_ref, route_ref):
    _, _, g1, sh2, sc2, _ = _mod_rows(mod_ref)
    out = _dot(o_ref[...], wo_ref[...])
    x3 = _layernorm(ALPHA * x2_ref[...] + (1.0 + g1) * out, lng_ref[...], lnb_ref[...])
    xm2 = x3 * (1.0 + sc2) + sh2
    x3_ref[...] = x3
    xm2_ref[...] = xm2
    route_ref[...] = _route(xm2, wr_ref, br_ref)


def _oproj(o, x2, mod, w_o, ln_g, ln_b, w_r, b_r, *, rows_per_batch, tm):
    t = x2.shape[0]
    tpb = rows_per_batch // tm
    row_spec = pl.BlockSpec((tm, D_MODEL), lambda i: (i, 0))
    return pl.pallas_call(
        _oproj_kernel,
        grid=(t // tm,),
        in_specs=[row_spec, row_spec,
                  pl.BlockSpec((None, 6, D_MODEL), lambda i: (i // tpb, 0, 0)),
                  _const_spec((D_MODEL, D_MODEL)),
                  _const_spec((1, D_MODEL)), _const_spec((1, D_MODEL)),
                  _const_spec((D_MODEL, LANES)), _const_spec((1, LANES))],
        out_specs=[row_spec, row_spec, pl.BlockSpec((ROUTE_ROWS, tm), lambda i: (0, i))],
        out_shape=[jax.ShapeDtypeStruct((t, D_MODEL), F32), jax.ShapeDtypeStruct((t, D_MODEL), F32),
                   jax.ShapeDtypeStruct((ROUTE_ROWS, t), F32)],
        compiler_params=_params(1),
    )(o, x2, mod, w_o, ln_g, ln_b, w_r, b_r)


def _final_ln_kernel(x_ref, y_ref, mod_ref, lng_ref, lnb_ref, o_ref):
    g2 = mod_ref[5:6, :]
    o_ref[...] = _layernorm(ALPHA * x_ref[...] + (1.0 + g2) * y_ref[...], lng_ref[...], lnb_ref[...])


def _final_ln(x, y, mod, ln_g, ln_b, *, rows_per_batch, tm):
    t = x.shape[0]
    tpb = rows_per_batch // tm
    row_spec = pl.BlockSpec((tm, D_MODEL), lambda i: (i, 0))
    return pl.pallas_call(
        _final_ln_kernel,
        grid=(t // tm,),
        in_specs=[row_spec, row_spec,
                  pl.BlockSpec((None, 6, D_MODEL), lambda i: (i // tpb, 0, 0)),
                  _const_spec((1, D_MODEL)), _const_spec((1, D_MODEL))],
        out_specs=row_spec,
        out_shape=jax.ShapeDtypeStruct((t, D_MODEL), F32),
        compiler_params=_params(1),
    )(x, y, mod, ln_g, ln_b)


def _router_weights(w_rg, b_rg, w_re, b_re):
    w = jnp.zeros((D_MODEL, LANES), F32)
    w = w.at[:, :N_GROUPS].set(w_rg).at[:, EXPERT_LOGIT_ROW:EXPERT_LOGIT_ROW + N_EXPERTS].set(w_re)
    b = jnp.zeros((1, LANES), F32)
    b = b.at[0, :N_GROUPS].set(b_rg).at[0, EXPERT_LOGIT_ROW:EXPERT_LOGIT_ROW + N_EXPERTS].set(b_re)
    return w, b


def kernel(x_prompt, x_sample, cache_k, cache_v, c_prompt, c_sample, ada_w, ada_b, ln_g, ln_b,
           a_w_in, a_vn_g, a_vn_b, a_w_s, a_b_s, a_w_out, kv_w_k, kv_w_v, b_w_q, b_rel, b_w_o,
           moe_w_rg, moe_b_rg, moe_w_re, moe_b_re, moe_w1, moe_w3, moe_w2):
    n_prompt, seq, _ = x_prompt.shape
    n_sample, dec_seq, _ = x_sample.shape
    cache_len = cache_k.shape[1]
    assert cache_len == BAND_PAST and dec_seq <= A_CHUNK and seq % BAND_PAST == 0

    n_cond = n_prompt + n_sample
    n_cond_pad = -(-n_cond // 8) * 8
    c_all = jnp.concatenate([c_prompt, c_sample,
                             jnp.zeros((n_cond_pad - n_cond, D_MODEL), F32)], axis=0)
    mod = _ada(c_all, ada_w, ada_b).reshape(DEPTH, n_cond_pad, 6, D_MODEL)

    w_in = a_w_in[0].astype(BF16)
    w_out = a_w_out[0].astype(BF16)
    w_k, w_v, w_q, w_o = (w.astype(BF16) for w in (kv_w_k, kv_w_v, b_w_q[0], b_w_o[0]))
    w1, w3, w2 = moe_w1.astype(BF16), moe_w3.astype(BF16), moe_w2.astype(BF16)
    routers = [_router_weights(moe_w_rg[l], moe_b_rg[l], moe_w_re[l], moe_b_re[l])
               for l in range(DEPTH)]
    lnp = lambda l, k: (ln_g[l, k].reshape(1, D_MODEL), ln_b[l, k].reshape(1, D_MODEL))
    b_s_t = jnp.transpose(a_b_s[0])

    def run(x, mod_l, *, rows_per_batch, tm, tm_moe, tq, keep, emit_va, attend):
        nb = x.shape[0] // rows_per_batch
        res = _mixer_a(x, mod_l[0], w_in, a_vn_g[0].reshape(1, A_WIDTH), a_vn_b[0].reshape(1, A_WIDTH),
                       a_w_s[0], b_s_t, w_out, *lnp(0, 0), *routers[0],
                       rows_per_batch=rows_per_batch, tm=tm, emit_va=emit_va)
        x1, xm2, route = res[:3]
        y = _hier_moe(xm2, route, w1[0], w3[0], w2[0], tm=tm_moe)
        x2, q, k, v, k32, v32 = _ln_qkv(x1, y, mod_l[0], mod_l[1], *lnp(0, 1), w_q, w_k, w_v,
                                        rows_per_batch=rows_per_batch, tm=tm, keep=keep)
        shape3 = (nb, rows_per_batch, D_MODEL)
        o = attend(q.reshape(shape3), k.reshape(shape3), v.reshape(shape3), tq)
        x3, xm2, route = _oproj(o.reshape(-1, D_MODEL), x2, mod_l[1], w_o, *lnp(1, 0), *routers[1],
                                rows_per_batch=rows_per_batch, tm=tm)
        y = _hier_moe(xm2, route, w1[1], w3[1], w2[1], tm=tm_moe)
        out = _final_ln(x3, y, mod_l[1], *lnp(1, 1), rows_per_batch=rows_per_batch, tm=tm)
        return out, k32, v32, (res[3] if emit_va else None)

    bias_prompt = _attn_bias(b_rel[0], n_valid_keys=None)
    bias_sample = _attn_bias(b_rel[0], n_valid_keys=cache_len + dec_seq)

    def attend_prompt(q, k, v, tq):
        return _attention(q, k, k, v, v, bias_prompt, tq=tq, mask_first=True)

    cache_kb = cache_k.reshape(n_sample, cache_len, D_MODEL).astype(BF16)
    cache_vb = cache_v.reshape(n_sample, cache_len, D_MODEL).astype(BF16)

    def attend_sample(q, k, v, tq):
        return _attention(q, cache_kb, k, cache_vb, v, bias_sample, tq=tq, mask_first=False)

    keep = min(BAND_PAST, seq)
    y_p, k_p, v_p, _ = run(x_prompt.reshape(-1, D_MODEL), mod[:, :n_prompt],
                           rows_per_batch=seq, tm=512, tm_moe=256, tq=512, keep=keep,
                           emit_va=False, attend=attend_prompt)

    x_s = jnp.pad(x_sample, ((0, 0), (0, A_CHUNK - dec_seq), (0, 0)))
    y_s, k_s, v_s, va_s = run(x_s.reshape(-1, D_MODEL), mod[:, n_prompt:n_cond],
                              rows_per_batch=A_CHUNK, tm=A_CHUNK, tm_moe=A_CHUNK, tq=A_CHUNK,
                              keep=A_CHUNK, emit_va=True, attend=attend_sample)

    y_prompt = y_p.reshape(n_prompt, seq, D_MODEL)
    y_sample = y_s.reshape(n_sample, A_CHUNK, D_MODEL)[:, :dec_seq]
    new_k_prompt = k_p.reshape(n_prompt, keep, N_HEADS, HEAD_DIM)
    new_v_prompt = v_p.reshape(n_prompt, keep, N_HEADS, HEAD_DIM)
    new_k_sample = k_s[:, :dec_seq].reshape(n_sample, dec_seq, N_HEADS, HEAD_DIM)
    new_v_sample = v_s[:, :dec_seq].reshape(n_sample, dec_seq, N_HEADS, HEAD_DIM)
    new_va_sample = va_s.reshape(n_sample, A_CHUNK, A_WIDTH)[None, :, :dec_seq]
    return (y_prompt, y_sample, new_k_prompt, new_v_prompt, new_k_sample, new_v_sample, new_va_sample)
```
